```python
import math
import jax
import jax.numpy as jnp
from jax import lax
import numpy as np

D_MODEL = 1024
BATCH = 4
SEQ = 4096
DEPTH = 2

N_A = DEPTH // 2
N_B = DEPTH - N_A
S5_GROUP = 16
S5_WIDTH = D_MODEL
S5_GROUPS = S5_WIDTH // S5_GROUP
S5_STATE = 64
HEAD_DIM = 64
N_HEADS = D_MODEL // HEAD_DIM
N_KV_HEADS = 4
Q_PER_KV = N_HEADS // N_KV_HEADS
WINDOW = 128
ATTN_BLOCK = 128
ROPE_THETA = 10000.0
PEER_HEADS = 8
PEER_KEY_DIM = 128
PEER_HALF = PEER_KEY_DIM // 2
PEER_N_KEYS = 128
PEER_N_EXPERTS = PEER_N_KEYS * PEER_N_KEYS
PEER_TOPK = 16
PEER_TOKEN_BLOCK = 128
N_COND = 6 * DEPTH + 4
RMS_EPS = 1e-6
NEG_INF = -1e30

kernel_name = "yoco_s5_swa_sink_peer_adaln"


def rmsnorm(x, g):
    x32 = x.astype(jnp.float32)
    y = x32 * lax.rsqrt(jnp.mean(x32 * x32, axis=-1, keepdims=True) + RMS_EPS)
    return (y * g.astype(jnp.float32)).astype(x.dtype)


def modulate(h, shift, scale):
    return h * (1 + scale[:, None, :]) + shift[:, None, :]


def rope_tables(positions):
    inv_freq = ROPE_THETA ** (-jnp.arange(0, HEAD_DIM, 2, dtype=jnp.float32) / HEAD_DIM)
    ang = positions.astype(jnp.float32)[..., None] * inv_freq
    return jnp.cos(ang)[:, :, None, :], jnp.sin(ang)[:, :, None, :]


def apply_rope(t, cos, sin):
    cos = cos.astype(t.dtype)
    sin = sin.astype(t.dtype)
    t1, t2 = jnp.split(t, 2, axis=-1)
    return jnp.concatenate([t1 * cos - t2 * sin, t2 * cos + t1 * sin], axis=-1)


def s5_discretize(a_re, a_im, log_dt, b_re, b_im):
    a_re = a_re.astype(jnp.float32)
    a_im = a_im.astype(jnp.float32)
    b_re = b_re.astype(jnp.float32)
    b_im = b_im.astype(jnp.float32)
    dt = jnp.exp(log_dt.astype(jnp.float32))[:, None]
    mag = jnp.exp(a_re * dt)
    lb_re = mag * jnp.cos(a_im * dt)
    lb_im = mag * jnp.sin(a_im * dt)
    nr = lb_re - 1.0
    ni = lb_im
    den = a_re * a_re + a_im * a_im
    coef_re = (nr * a_re + ni * a_im) / den
    coef_im = (ni * a_re - nr * a_im) / den
    bb_re = coef_re[..., None] * b_re - coef_im[..., None] * b_im
    bb_im = coef_re[..., None] * b_im + coef_im[..., None] * b_re
    return lb_re, lb_im, bb_re, bb_im


def s5_scan_one(u, lb_re, lb_im, bb_re, bb_im, c_re, c_im, d):
    bu_re = jnp.einsum('gpc,sgc->sgp', bb_re, u)
    bu_im = jnp.einsum('gpc,sgc->sgp', bb_im, u)
    a_re = jnp.broadcast_to(lb_re, bu_re.shape)
    a_im = jnp.broadcast_to(lb_im, bu_im.shape)

    def combine(e1, e2):
        a1r, a1i, b1r, b1i = e1
        a2r, a2i, b2r, b2i = e2
        return (a2r * a1r - a2i * a1i,
                a2r * a1i + a2i * a1r,
                a2r * b1r - a2i * b1i + b2r,
                a2r * b1i + a2i * b1r + b2i)

    _, _, xr, xi = lax.associative_scan(combine, (a_re, a_im, bu_re, bu_im), axis=0)
    y = jnp.einsum('gcp,sgp->sgc', c_re, xr) - jnp.einsum('gcp,sgp->sgc', c_im, xi)
    return y + d * u


def s5_mixer(h, w_in, a_re, a_im, log_dt, b_re, b_im, c_re, c_im, d, w_glu, b_glu, w_out):
    B, S, _ = h.shape
    u = (h @ w_in).astype(jnp.float32).reshape(B, S, S5_GROUPS, S5_GROUP)
    lb_re, lb_im, bb_re, bb_im = s5_discretize(a_re, a_im, log_dt, b_re, b_im)
    c_re32 = c_re.astype(jnp.float32)
    c_im32 = c_im.astype(jnp.float32)
    d32 = d.astype(jnp.float32)
    y = lax.map(lambda ub: s5_scan_one(ub, lb_re, lb_im, bb_re, bb_im, c_re32, c_im32, d32), u)
    y = jax.nn.gelu(y.reshape(B, S, S5_WIDTH))
    y = y * jax.nn.sigmoid(y @ w_glu.astype(jnp.float32) + b_glu.astype(jnp.float32))
    return (y.astype(h.dtype) @ w_out).astype(h.dtype)


def shared_kv(x, g, shift, scale, w_kv, cos, sin):
    B, S, _ = x.shape
    h = modulate(rmsnorm(x, g), shift, scale)
    kv = h @ w_kv
    k, v = jnp.split(kv, 2, axis=-1)
    k = apply_rope(k.reshape(B, S, N_KV_HEADS, HEAD_DIM), cos, sin)
    v = v.reshape(B, S, N_KV_HEADS, HEAD_DIM)
    return k, v


def window_attention(h, k, v, w_q, sinks, w_o, cos, sin):
    B, S, _ = h.shape
    nb = S // ATTN_BLOCK
    q = apply_rope((h @ w_q).reshape(B, S, N_HEADS, HEAD_DIM), cos, sin)
    q = q.reshape(B, nb, ATTN_BLOCK, N_KV_HEADS, Q_PER_KV, HEAD_DIM)

    def band(t):
        tp = jnp.pad(t, ((0, 0), (ATTN_BLOCK, 0), (0, 0), (0, 0)))
        tp = tp.reshape(B, nb + 1, ATTN_BLOCK, N_KV_HEADS, HEAD_DIM)
        return jnp.concatenate([tp[:, :-1], tp[:, 1:]], axis=2)

    kb = band(k)
    vb = band(v)
    qi = jnp.arange(ATTN_BLOCK)[:, None]
    kj = jnp.arange(2 * ATTN_BLOCK)[None, :]
    diff = qi + ATTN_BLOCK - kj
    k_abs = (jnp.arange(nb)[:, None, None] - 1) * ATTN_BLOCK + kj[None]
    mask = (diff >= 0)[None] & (diff < WINDOW)[None] & (k_abs >= 0)
    s = jnp.einsum('bnqkgd,bnjkd->bnkgqj', q, kb).astype(jnp.float32) * (HEAD_DIM ** -0.5)
    s = jnp.where(mask[None, :, None, None, :, :], s, NEG_INF)
    sink = sinks.astype(jnp.float32).reshape(N_KV_HEADS, Q_PER_KV)[None, None, :, :, None, None]
    m = jnp.maximum(jnp.max(s, axis=-1, keepdims=True), sink)
    p = jnp.exp(s - m)
    p = p / (jnp.sum(p, axis=-1, keepdims=True) + jnp.exp(sink - m))
    o = jnp.einsum('bnkgqj,bnjkd->bnqkgd', p.astype(vb.dtype), vb)
    return o.reshape(B, S, N_HEADS * HEAD_DIM) @ w_o


def peer(h, w_q, sub_keys, u_tab, v_tab):
    B, S, D = h.shape
    T = B * S
    xt = h.reshape(T, D)
    q = (xt @ w_q).reshape(T, PEER_HEADS, 2, PEER_HALF)
    sc = jnp.einsum('thpd,hpnd->thpn', q, sub_keys).astype(jnp.float32)
    s_half, i_half = lax.top_k(sc, PEER_TOPK)
    cand = s_half[:, :, 0, :, None] + s_half[:, :, 1, None, :]
    cand_id = i_half[:, :, 0, :, None] * PEER_N_KEYS + i_half[:, :, 1, None, :]
    cand = cand.reshape(T, PEER_HEADS, PEER_TOPK * PEER_TOPK)
    cand_id = cand_id.reshape(T, PEER_HEADS, PEER_TOPK * PEER_TOPK)
    top_s, top_pos = lax.top_k(cand, PEER_TOPK)
    ids = jnp.take_along_axis(cand_id, top_pos, axis=-1)
    gates = jax.nn.softmax(top_s, axis=-1).astype(h.dtype)
    nblk = T // PEER_TOKEN_BLOCK
    ids = ids.reshape(nblk, PEER_TOKEN_BLOCK, PEER_HEADS * PEER_TOPK)
    gates = gates.reshape(nblk, PEER_TOKEN_BLOCK, PEER_HEADS * PEER_TOPK)
    xb = xt.reshape(nblk, PEER_TOKEN_BLOCK, D)

    def expert_block(args):
        xc, idc, gc = args
        act = jax.nn.gelu(jnp.einsum('td,tkd->tk', xc, u_tab[idc])) * gc
        return jnp.einsum('tk,tkd->td', act, v_tab[idc])

    out = lax.map(expert_block, (xb, ids, gates))
    return out.reshape(B, S, D)


def setup_inputs(seed: int = 0) -> dict:
    key = jax.random.key(seed)
    ks = jax.random.split(key, 32)
    f32 = jnp.float32
    B, S, D = BATCH, SEQ, D_MODEL

    def nrm(k, shape, s):
        return jax.random.normal(k, shape, f32) * s

    G, P, C = S5_GROUPS, S5_STATE, S5_GROUP
    n = jnp.arange(P, dtype=f32)
    return {
        "x": nrm(ks[0], (B, S, D), 1.0),
        "c": nrm(ks[1], (B, D), 1.0),
        "positions": jnp.broadcast_to(jnp.arange(S, dtype=jnp.int32)[None, :], (B, S)),
        "w_cond": nrm(ks[2], (D, N_COND * D), 0.5 * D ** -0.5),
        "b_cond": nrm(ks[3], (N_COND * D,), 0.02),
        "norm_mix": 1.0 + nrm(ks[4], (DEPTH, D), 0.02),
        "norm_ffn": 1.0 + nrm(ks[5], (DEPTH, D), 0.02),
        "s5_w_in": nrm(ks[6], (N_A, D, S5_WIDTH), D ** -0.5),
        "s5_a_re": -0.5 + nrm(ks[7], (N_A, G, P), 0.01),
        "s5_a_im": math.pi * n + nrm(ks[8], (N_A, G, P), 0.01),
        "s5_log_dt": jax.random.uniform(ks[9], (N_A, G), f32, math.log(1e-3), math.log(1e-1)),
        "s5_b_re": nrm(ks[10], (N_A, G, P, C), (2 * C) ** -0.5),
        "s5_b_im": nrm(ks[11], (N_A, G, P, C), (2 * C) ** -0.5),
        "s5_c_re": nrm(ks[12], (N_A, G, C, P), P ** -0.5),
        "s5_c_im": nrm(ks[13], (N_A, G, C, P), P ** -0.5),
        "s5_d": nrm(ks[14], (N_A, G, C), 1.0),
        "s5_w_glu": nrm(ks[15], (N_A, S5_WIDTH, S5_WIDTH), S5_WIDTH ** -0.5),
        "s5_b_glu": nrm(ks[16], (N_A, S5_WIDTH), 0.02),
        "s5_w_out": nrm(ks[17], (N_A, S5_WIDTH, D), S5_WIDTH ** -0.5),
        "kv_norm": 1.0 + nrm(ks[18], (D,), 0.02),
        "w_kv": nrm(ks[19], (D, 2 * N_KV_HEADS * HEAD_DIM), D ** -0.5),
        "attn_w_q": nrm(ks[20], (N_B, D, N_HEADS * HEAD_DIM), D ** -0.5),
        "attn_sinks": nrm(ks[21], (N_B, N_HEADS), 0.5),
        "attn_w_o": nrm(ks[22], (N_B, N_HEADS * HEAD_DIM, D), (N_HEADS * HEAD_DIM) ** -0.5),
        "peer_w_q": nrm(ks[23], (DEPTH, D, PEER_HEADS * PEER_KEY_DIM), D ** -0.5),
        "peer_sub_keys": nrm(ks[24], (DEPTH, PEER_HEADS, 2, PEER_N_KEYS, PEER_HALF), PEER_HALF ** -0.5),
        "peer_u": nrm(ks[25], (DEPTH, PEER_N_EXPERTS, D), D ** -0.5),
        "peer_v": nrm(ks[26], (DEPTH, PEER_N_EXPERTS, D), 1.0),
        "final_norm": 1.0 + nrm(ks[27], (D,), 0.02),
    }


def reference(x, c, positions, w_cond, b_cond, norm_mix, norm_ffn, s5_w_in, s5_a_re, s5_a_im,
              s5_log_dt, s5_b_re, s5_b_im, s5_c_re, s5_c_im, s5_d, s5_w_glu, s5_b_glu, s5_w_out,
              kv_norm, w_kv, attn_w_q, attn_sinks, attn_w_o, peer_w_q, peer_sub_keys, peer_u,
              peer_v, final_norm):
    B = x.shape[0]
    cond = (jax.nn.silu(c.astype(jnp.float32)) @ w_cond.astype(jnp.float32)
            + b_cond.astype(jnp.float32)).astype(x.dtype).reshape(B, N_COND, D_MODEL)
    cos, sin = rope_tables(positions)
    k_sh = None
    v_sh = None
    for i in range(DEPTH):
        m = cond[:, 6 * i:6 * i + 6]
        h = modulate(rmsnorm(x, norm_mix[i]), m[:, 0], m[:, 1])
        if i < N_A:
            a = i
            y = s5_mixer(h, s5_w_in[a], s5_a_re[a], s5_a_im[a], s5_log_dt[a], s5_b_re[a], s5_b_im[a],
                         s5_c_re[a], s5_c_im[a], s5_d[a], s5_w_glu[a], s5_b_glu[a], s5_w_out[a])
        else:
            b = i - N_A
            y = window_attention(h, k_sh, v_sh, attn_w_q[b], attn_sinks[b], attn_w_o[b], cos, sin)
        x = x + m[:, 2][:, None, :] * y
        h = modulate(rmsnorm(x, norm_ffn[i]), m[:, 3], m[:, 4])
        x = x + m[:, 5][:, None, :] * peer(h, peer_w_q[i], peer_sub_keys[i], peer_u[i], peer_v[i])
        if i == N_A - 1:
            k_sh, v_sh = shared_kv(x, kv_norm, cond[:, 6 * DEPTH], cond[:, 6 * DEPTH + 1], w_kv, cos, sin)
    return modulate(rmsnorm(x, final_norm), cond[:, 6 * DEPTH + 2], cond[:, 6 * DEPTH + 3])
```

```python
import functools
import math

import jax
import jax.numpy as jnp
from jax import lax
from jax.experimental import pallas as pl
from jax.experimental.pallas import tpu as pltpu

F32 = jnp.float32
BF16 = jnp.bfloat16

S5_GROUP = 16
S5_STATE = 64
HEAD_DIM = 64
N_KV_HEADS = 4
WINDOW = 128
ROPE_THETA = 10000.0
PEER_HEADS = 8
PEER_HALF = 64
PEER_N_KEYS = 128
PEER_TOPK = 16
N_COND_PER_LAYER = 6
RMS_EPS = 1e-6
NEG_INF = -1e30
NEG_BIG = -3.0e38

LANES = 128
SUBLANES = 8
VMEM_LIMIT_BYTES = 56 * 1024 * 1024
S5_CHUNK = 16
TOKEN_TILE = 512
ROUTE_TILE = 256
EXPERT_TOKEN_TILE = 512
EXPERT_I1_PER_STEP = 4
ATTN_TILE = 512

_NT = (((1,), (1,)), ((), ()))
_TN = (((0,), (0,)), ((), ()))


def _cparams(sem):
    return pltpu.CompilerParams(dimension_semantics=sem, vmem_limit_bytes=VMEM_LIMIT_BYTES)


def _norm_mod(x, g, shift, scale):
    ms = jnp.mean(x * x, axis=-1, keepdims=True)
    y = (x * lax.rsqrt(ms + RMS_EPS)) * g
    return y * (1.0 + scale) + shift


def _cond_kernel(c_ref, w_ref, b_ref, o_ref):
    c = c_ref[...]
    a = c * jax.nn.sigmoid(c)
    o_ref[...] = jnp.dot(a, w_ref[...], preferred_element_type=F32,
                         precision=lax.Precision.HIGHEST) + b_ref[...]


def _cond_proj(c, w_cond, b_cond):
    B, D = c.shape
    N = w_cond.shape[1]
    rows = -(-B // SUBLANES) * SUBLANES
    cp = jnp.zeros((rows, D), F32).at[:B].set(c.astype(F32))
    tn = 2048
    out = pl.pallas_call(
        _cond_kernel,
        grid=(N // tn,),
        in_specs=[pl.BlockSpec((rows, D), lambda j: (0, 0)),
                  pl.BlockSpec((D, tn), lambda j: (0, j)),
                  pl.BlockSpec((1, tn), lambda j: (0, j))],
        out_specs=pl.BlockSpec((rows, tn), lambda j: (0, j)),
        out_shape=jax.ShapeDtypeStruct((rows, N), F32),
        compiler_params=_cparams(("parallel",)),
        name="cond_proj",
    )(cp, w_cond.astype(F32), b_cond.astype(F32).reshape(1, N))
    return out[:B]


def _norm_proj_kernel(*refs, shift_row, scale_row, n_rope, out_widths):
    if n_rope:
        x_ref, cond_ref, g_ref, w_ref, cos_ref, sin_ref = refs[:6]
        out_refs = refs[6:]
    else:
        x_ref, cond_ref, g_ref, w_ref = refs[:4]
        out_refs = refs[4:]
    h = _norm_mod(x_ref[0], g_ref[...], cond_ref[0, shift_row:shift_row + 1, :],
                  cond_ref[0, scale_row:scale_row + 1, :])
    acc = jnp.dot(h.astype(BF16), w_ref[...], preferred_element_type=F32)
    col = 0
    if n_rope:
        reps = n_rope // LANES
        cosf = jnp.tile(cos_ref[0], (1, reps))
        sinf = jnp.tile(sin_ref[0], (1, reps))
        r = acc[:, :n_rope] * cosf + acc[:, n_rope:2 * n_rope] * sinf
        out_refs[0][0] = r.astype(out_refs[0].dtype)
        col = 2 * n_rope
        rest = out_refs[1:]
        widths = out_widths[1:]
    else:
        rest = out_refs
        widths = out_widths
    for o_ref, wdt in zip(rest, widths):
        o_ref[0] = acc[:, col:col + wdt].astype(o_ref.dtype)
        col += wdt


def _norm_proj(x, cond, g, w, shift_row, scale_row, out_widths, out_dtypes, rope=None, name="norm_proj"):
    B, S, D = x.shape
    N = w.shape[1]
    tm = min(TOKEN_TILE, S)
    n_rope = out_widths[0] if rope is not None else 0
    in_specs = [pl.BlockSpec((1, tm, D), lambda b, i: (b, i, 0)),
                pl.BlockSpec((1, cond.shape[1], D), lambda b, i: (b, 0, 0)),
                pl.BlockSpec((1, D), lambda b, i: (0, 0)),
                pl.BlockSpec((D, N), lambda b, i: (0, 0))]
    args = [x, cond, g.reshape(1, D), w]
    if rope is not None:
        in_specs += [pl.BlockSpec((1, tm, LANES), lambda b, i: (b, i, 0))] * 2
        args += list(rope)
    outs = pl.pallas_call(
        functools.partial(_norm_proj_kernel, shift_row=shift_row, scale_row=scale_row,
                          n_rope=n_rope, out_widths=tuple(out_widths)),
        grid=(B, S // tm),
        in_specs=in_specs,
        out_specs=[pl.BlockSpec((1, tm, wdt), lambda b, i: (b, i, 0)) for wdt in out_widths],
        out_shape=[jax.ShapeDtypeStruct((B, S, wdt), dt) for wdt, dt in zip(out_widths, out_dtypes)],
        compiler_params=_cparams(("parallel", "parallel")),
        name=name,
    )(*args)
    return outs


def _s5_core_kernel(u_ref, t_ref, pin_ref, q_ref, pa_ref, pb_ref, y_ref, *, chunks_per_seq):
    u = u_ref[0]
    x = jnp.dot(u, pin_ref[0], preferred_element_type=F32)
    rows = x.shape[0]
    k = lax.broadcasted_iota(jnp.int32, (rows, 1), 0) % chunks_per_seq
    half = x.shape[1] // 2
    n_steps = pa_ref.shape[1]
    for si in range(n_steps):
        d = 1 << si
        if d >= chunks_per_seq:
            break
        sh = jnp.where(k >= d, pltpu.roll(x, d, axis=0), 0.0)
        x = x + sh * pa_ref[0, si:si + 1, :] + pltpu.roll(sh, half, axis=1) * pb_ref[0, si:si + 1, :]
    xprev = jnp.where(k >= 1, pltpu.roll(x, 1, axis=0), 0.0)
    hi = xprev.astype(BF16)
    lo = (xprev - hi.astype(F32)).astype(BF16)
    y = jnp.dot(u, t_ref[0], preferred_element_type=F32)
    y = y + jnp.dot(hi, q_ref[0], preferred_element_type=F32)
    y = y + jnp.dot(lo, q_ref[0], preferred_element_type=F32)
    y_ref[0] = y


def _s5_tables(a_re, a_im, log_dt, b_re, b_im, c_re, c_im, chunks_per_seq):
    hp = lax.Precision.HIGHEST
    L = S5_CHUNK
    a_re = a_re.astype(F32)
    a_im = a_im.astype(F32)
    b_re = b_re.astype(F32)
    b_im = b_im.astype(F32)
    c_re = c_re.astype(F32)
    c_im = c_im.astype(F32)
    G, P = a_re.shape
    C = b_re.shape[-1]
    dt = jnp.exp(log_dt.astype(F32))[:, None]
    mag = jnp.exp(a_re * dt)
    lb_re = mag * jnp.cos(a_im * dt)
    lb_im = mag * jnp.sin(a_im * dt)
    nr = lb_re - 1.0
    ni = lb_im
    den = a_re * a_re + a_im * a_im
    coef_re = (nr * a_re + ni * a_im) / den
    coef_im = (ni * a_re - nr * a_im) / den
    bb_re = coef_re[..., None] * b_re - coef_im[..., None] * b_im
    bb_im = coef_re[..., None] * b_im + coef_im[..., None] * b_re

    def powers(ks):
        ks = jnp.asarray(ks, F32)[:, None, None]
        m = jnp.exp(ks * (a_re * dt)[None])
        ang = ks * (a_im * dt)[None]
        return m * jnp.cos(ang), m * jnp.sin(ang)

    pw_re, pw_im = powers(list(range(L + 1)))
    cl_re = c_re[:, None] * pw_re[:L].transpose(1, 0, 2)[:, :, None, :] \
        - c_im[:, None] * pw_im[:L].transpose(1, 0, 2)[:, :, None, :]
    cl_im = c_re[:, None] * pw_im[:L].transpose(1, 0, 2)[:, :, None, :] \
        + c_im[:, None] * pw_re[:L].transpose(1, 0, 2)[:, :, None, :]
    kern = jnp.einsum('gkop,gpi->gkoi', cl_re, bb_re, precision=hp) \
        - jnp.einsum('gkop,gpi->gkoi', cl_im, bb_im, precision=hp)
    kfull = jnp.concatenate([jnp.zeros((G, L - 1, C, C), F32), kern], axis=1)
    s_idx = jnp.arange(L)[:, None]
    t_idx = jnp.arange(L)[None, :]
    toep = kfull[:, (t_idx - s_idx) + (L - 1)]
    toep = toep.transpose(0, 1, 4, 2, 3).reshape(G, L * C, L * C)
    rev_re = pw_re[:L][::-1].transpose(1, 0, 2)
    rev_im = pw_im[:L][::-1].transpose(1, 0, 2)
    pin_re = rev_re[:, :, None, :] * bb_re.transpose(0, 2, 1)[:, None] \
        - rev_im[:, :, None, :] * bb_im.transpose(0, 2, 1)[:, None]
    pin_im = rev_re[:, :, None, :] * bb_im.transpose(0, 2, 1)[:, None] \
        + rev_im[:, :, None, :] * bb_re.transpose(0, 2, 1)[:, None]
    pin = jnp.concatenate([pin_re, pin_im], axis=-1).reshape(G, L * C, 2 * P)
    nx_re = pw_re[1:L + 1].transpose(1, 0, 2)
    nx_im = pw_im[1:L + 1].transpose(1, 0, 2)
    q_re = c_re[:, None] * nx_re[:, :, None, :] - c_im[:, None] * nx_im[:, :, None, :]
    q_im = c_re[:, None] * nx_im[:, :, None, :] + c_im[:, None] * nx_re[:, :, None, :]
    qout = jnp.concatenate([q_re, -q_im], axis=-1)
    qout = qout.transpose(0, 3, 1, 2).reshape(G, 2 * P, L * C)
    n_steps = max(1, int(math.ceil(math.log2(chunks_per_seq))))
    sp_re, sp_im = powers([L * (1 << s) for s in range(n_steps)])
    pa = jnp.concatenate([sp_re, sp_re], axis=-1).transpose(1, 0, 2)
    pb = jnp.concatenate([-sp_im, sp_im], axis=-1).transpose(1, 0, 2)
    return toep.astype(BF16), pin.astype(BF16), qout.astype(BF16), pa, pb


def _s5_core(u, tables):
    toep, pin, qout, pa, pb = tables
    B, S, W = u.shape
    G = W // S5_GROUP
    L = S5_CHUNK
    LC = L * S5_GROUP
    rows = B * S // L
    ug = u.astype(BF16).reshape(rows, L, G, S5_GROUP).transpose(2, 0, 1, 3).reshape(G, rows, LC)
    P2 = pin.shape[-1]
    ns = pa.shape[1]
    yg = pl.pallas_call(
        functools.partial(_s5_core_kernel, chunks_per_seq=S // L),
        grid=(G,),
        in_specs=[pl.BlockSpec((1, rows, LC), lambda g: (g, 0, 0)),
                  pl.BlockSpec((1, LC, LC), lambda g: (g, 0, 0)),
                  pl.BlockSpec((1, LC, P2), lambda g: (g, 0, 0)),
                  pl.BlockSpec((1, P2, LC), lambda g: (g, 0, 0)),
                  pl.BlockSpec((1, ns, P2), lambda g: (g, 0, 0)),
                  pl.BlockSpec((1, ns, P2), lambda g: (g, 0, 0))],
        out_specs=pl.BlockSpec((1, rows, LC), lambda g: (g, 0, 0)),
        out_shape=jax.ShapeDtypeStruct((G, rows, LC), F32),
        compiler_params=_cparams(("parallel",)),
        name="s5_core",
    )(ug, toep, pin, qout, pa, pb)
    return yg.reshape(G, rows, L, S5_GROUP).transpose(1, 2, 0, 3).reshape(B, S, W)


def _s5_tail_kernel(y_ref, u_ref, x_ref, cond_ref, d_ref, wg_ref, bg_ref, wo_ref, g_ref,
                    xo_ref, h_ref, *, gate_row, shift_row, scale_row):
    y = jax.nn.gelu(y_ref[0] + d_ref[...] * u_ref[0])
    z = jnp.dot(y.astype(BF16), wg_ref[...], preferred_element_type=F32) + bg_ref[...]
    y = y * jax.nn.sigmoid(z)
    o = jnp.dot(y.astype(BF16), wo_ref[...], preferred_element_type=F32)
    xn = x_ref[0] + cond_ref[0, gate_row:gate_row + 1, :] * o
    xo_ref[0] = xn
    h_ref[0] = _norm_mod(xn, g_ref[...], cond_ref[0, shift_row:shift_row + 1, :],
                         cond_ref[0, scale_row:scale_row + 1, :]).astype(h_ref.dtype)


def _s5_tail(y, u, x, cond, d, w_glu, b_glu, w_out, g_ffn, gate_row, shift_row, scale_row):
    B, S, D = x.shape
    W = y.shape[-1]
    tm = min(TOKEN_TILE, S)
    tok = lambda w: pl.BlockSpec((1, tm, w), lambda b, i: (b, i, 0))
    const = lambda r, c: pl.BlockSpec((r, c), lambda b, i: (0, 0))
    return pl.pallas_call(
        functools.partial(_s5_tail_kernel, gate_row=gate_row, shift_row=shift_row, scale_row=scale_row),
        grid=(B, S // tm),
        in_specs=[tok(W), tok(W), tok(D),
                  pl.BlockSpec((1, cond.shape[1], D), lambda b, i: (b, 0, 0)),
                  const(1, W), const(W, W), const(1, W), const(W, D), const(1, D)],
        out_specs=[tok(D), tok(D)],
        out_shape=[jax.ShapeDtypeStruct((B, S, D), F32), jax.ShapeDtypeStruct((B, S, D), BF16)],
        compiler_params=_cparams(("parallel", "parallel")),
        name="s5_tail",
    )(y, u, x, cond, d.astype(F32).reshape(1, W), w_glu.astype(BF16), b_glu.astype(F32).reshape(1, W),
      w_out.astype(BF16), g_ffn.reshape(1, D))


def _extract_top(w, n, v_scr, rank=False):
    rk = jnp.full(w.shape, float(n), F32) if rank else None
    for k in range(n):
        m = jnp.max(w, axis=0, keepdims=True)
        v_scr[k:k + 1, :] = m
        hit = w == m
        w = jnp.where(hit, NEG_BIG, w)
        if rank:
            rk = jnp.where(hit, float(k), rk)
    return rk


def _peer_route_kernel(h_ref, wq_ref, keys_ref, rk2_ref, e2_ref, c1_ref, e1_ref,
                       q_scr, v1_scr, v2_scr):
    K = PEER_TOPK
    q_scr[...] = lax.dot_general(wq_ref[...], h_ref[...], _NT, preferred_element_type=F32)
    tn = h_ref.shape[0]
    row8 = lax.broadcasted_iota(jnp.int32, (SUBLANES, tn), 0)

    def head(hd, carry):
        o1 = pl.multiple_of(hd * (2 * PEER_HALF), 2 * PEER_HALF)
        o2 = pl.multiple_of(hd * (2 * PEER_HALF) + PEER_HALF, PEER_HALF)
        sc1 = jnp.dot(keys_ref[2 * hd], q_scr[pl.ds(o1, PEER_HALF), :].astype(BF16),
                      preferred_element_type=F32)
        sc2 = jnp.dot(keys_ref[2 * hd + 1], q_scr[pl.ds(o2, PEER_HALF), :].astype(BF16),
                      preferred_element_type=F32)
        _extract_top(sc1, K, v1_scr)
        rk2 = _extract_top(sc2, K, v2_scr, rank=True)
        v1 = v1_scr[...]
        v2 = v2_scr[...]
        slabs = [v1[0:1] + v2,
                 v1[1:2] + v2[0:8],
                 v1[8:16] + v2[0:1]]
        for b, (lo, hi) in enumerate(((2, 8), (2, 8), (2, 5), (2, 4), (2, 3))):
            s = v1[0:8] + v2[b:b + 1]
            slabs.append(jnp.where((row8 >= lo) & (row8 < hi), s, NEG_BIG))
        cand = jnp.concatenate(slabs, axis=0)
        top = None
        z = None
        m = None
        for k in range(K):
            m = jnp.max(cand, axis=0, keepdims=True)
            if k == 0:
                top = m
                z = jnp.ones_like(m)
            else:
                z = z + jnp.exp(m - top)
            cand = jnp.where(cand == m, NEG_BIG, cand)
        tau = m
        inv_z = 1.0 / z
        c1 = jnp.zeros_like(sc1)
        for a in range(K):
            cnt = jnp.sum(((v1[a:a + 1] + v2) >= tau).astype(F32), axis=0, keepdims=True)
            c1 = jnp.where(sc1 == v1[a:a + 1], cnt, c1)
        rk2_ref[hd] = rk2.astype(rk2_ref.dtype)
        c1_ref[hd] = c1.astype(c1_ref.dtype)
        e1_ref[hd] = jnp.exp(sc1 - v1[0:1]).astype(e1_ref.dtype)
        e2_ref[hd] = (jnp.exp(sc2 - v2[0:1]) * inv_z).astype(e2_ref.dtype)
        return carry

    lax.fori_loop(0, PEER_HEADS, head, 0)


def _peer_route(h, w_q, sub_keys):
    T, D = h.shape
    NQ = w_q.shape[1]
    tn = min(ROUTE_TILE, T)
    wq_t = w_q.astype(BF16).T
    keys = sub_keys.astype(BF16).reshape(2 * PEER_HEADS, PEER_N_KEYS, PEER_HALF)
    shp = jax.ShapeDtypeStruct((PEER_HEADS, PEER_N_KEYS, T), F32)
    ospec = pl.BlockSpec((PEER_HEADS, PEER_N_KEYS, tn), lambda j: (0, 0, j))
    return pl.pallas_call(
        _peer_route_kernel,
        grid=(T // tn,),
        in_specs=[pl.BlockSpec((tn, D), lambda j: (j, 0)),
                  pl.BlockSpec((NQ, D), lambda j: (0, 0)),
                  pl.BlockSpec((2 * PEER_HEADS, PEER_N_KEYS, PEER_HALF), lambda j: (0, 0, 0))],
        out_specs=[ospec, ospec, ospec, ospec],
        out_shape=[shp, shp, shp, shp],
        scratch_shapes=[pltpu.VMEM((NQ, tn), F32),
                        pltpu.VMEM((PEER_TOPK, tn), F32),
                        pltpu.VMEM((PEER_TOPK, tn), F32)],
        compiler_params=_cparams(("parallel",)),
        name="peer_route",
    )(h, wq_t, keys)


def _peer_expert_kernel(h_ref, u_ref, v_ref, rk2_ref, e2_ref, c1_ref, e1_ref, x_ref, cond_ref,
                        o_ref, acc_ref, *, n_i1, gate_row):
    i = pl.program_id(1)

    @pl.when(i == 0)
    def _():
        acc_ref[...] = jnp.zeros_like(acc_ref)

    z = lax.dot_general(u_ref[...], h_ref[...], _NT, preferred_element_type=F32)
    act = jax.nn.gelu(z)
    pieces = []
    for jj in range(n_i1):
        i1 = i * n_i1 + jj
        g = None
        for hd in range(PEER_HEADS):
            cnt = c1_ref[hd, pl.ds(i1, 1), :]
            e1 = e1_ref[hd, pl.ds(i1, 1), :]
            t = jnp.where(rk2_ref[hd] < cnt, e2_ref[hd], 0.0) * e1
            g = t if g is None else g + t
        pieces.append((act[jj * PEER_N_KEYS:(jj + 1) * PEER_N_KEYS] * g).astype(BF16))
    a = jnp.concatenate(pieces, axis=0)
    acc_ref[...] += lax.dot_general(a, v_ref[...], _TN, preferred_element_type=F32)

    @pl.when(i == pl.num_programs(1) - 1)
    def _():
        o_ref[...] = x_ref[...] + cond_ref[0, gate_row:gate_row + 1, :] * acc_ref[...]


def _peer_expert(h, x, cond, routing, u_tab, v_tab, gate_row, seq_len):
    T, D = h.shape
    E = u_tab.shape[0]
    rk2, e2, c1, e1 = routing
    tn = min(EXPERT_TOKEN_TILE, seq_len)
    n_i1 = EXPERT_I1_PER_STEP
    te = n_i1 * PEER_N_KEYS
    tiles_per_seq = seq_len // tn
    rspec = pl.BlockSpec((PEER_HEADS, PEER_N_KEYS, tn), lambda j, i: (0, 0, j))
    return pl.pallas_call(
        functools.partial(_peer_expert_kernel, n_i1=n_i1, gate_row=gate_row),
        grid=(T // tn, E // te),
        in_specs=[pl.BlockSpec((tn, D), lambda j, i: (j, 0)),
                  pl.BlockSpec((te, D), lambda j, i: (i, 0)),
                  pl.BlockSpec((te, D), lambda j, i: (i, 0)),
                  rspec, rspec, rspec, rspec,
                  pl.BlockSpec((tn, D), lambda j, i: (j, 0)),
                  pl.BlockSpec((1, cond.shape[1], D), lambda j, i: (j // tiles_per_seq, 0, 0))],
        out_specs=pl.BlockSpec((tn, D), lambda j, i: (j, 0)),
        out_shape=jax.ShapeDtypeStruct((T, D), F32),
        scratch_shapes=[pltpu.VMEM((tn, D), F32)],
        compiler_params=_cparams(("parallel", "arbitrary")),
        name="peer_expert",
    )(h, u_tab, v_tab, rk2, e2, c1, e1, x, cond)


def _peer_layer(h, x, cond, w_q, sub_keys, u_tab, v_tab, gate_row):
    B, S, D = x.shape
    hf = h.reshape(B * S, D)
    routing = _peer_route(hf, w_q, sub_keys)
    out = _peer_expert(hf, x.reshape(B * S, D), cond, routing, u_tab.astype(BF16), v_tab.astype(BF16),
                       gate_row, S)
    return out.reshape(B, S, D)


def _attn_kernel(sink_ref, q_ref, kc_ref, kp_ref, vc_ref, vp_ref, o_ref, *, n_heads):
    i = pl.program_id(1)
    tq = q_ref.shape[1]
    blk = WINDOW
    qpk = n_heads // N_KV_HEADS
    qi = lax.broadcasted_iota(jnp.int32, (qpk * blk, 2 * blk), 0) & (blk - 1)
    kj = lax.broadcasted_iota(jnp.int32, (qpk * blk, 2 * blk), 1)
    diff = qi + blk - kj
    band = (diff >= 0) & (diff < WINDOW)
    first = band & ((kj >= blk) | (i > 0))
    scale = HEAD_DIM ** -0.5
    for jb in range(tq // blk):
        mask = first if jb == 0 else band
        outs = []
        for kh in range(N_KV_HEADS):
            ks = slice(kh * HEAD_DIM, (kh + 1) * HEAD_DIM)
            if jb == 0:
                kb = jnp.concatenate([kp_ref[0, :, ks], kc_ref[0, 0:blk, ks]], axis=0)
                vb = jnp.concatenate([vp_ref[0, :, ks], vc_ref[0, 0:blk, ks]], axis=0)
            else:
                kb = kc_ref[0, (jb - 1) * blk:(jb + 1) * blk, ks]
                vb = vc_ref[0, (jb - 1) * blk:(jb + 1) * blk, ks]
            qg = jnp.concatenate(
                [q_ref[0, jb * blk:(jb + 1) * blk, (kh * qpk + g) * HEAD_DIM:(kh * qpk + g + 1) * HEAD_DIM]
                 for g in range(qpk)], axis=0)
            s = lax.dot_general(qg, kb, _NT, preferred_element_type=F32) * scale
            s = jnp.where(mask, s, NEG_INF)
            sink = jnp.concatenate(
                [jnp.full((blk, 1), sink_ref[kh * qpk + g], F32) for g in range(qpk)], axis=0)
            m = jnp.maximum(jnp.max(s, axis=-1, keepdims=True), sink)
            p = jnp.exp(s - m)
            p = p / (jnp.sum(p, axis=-1, keepdims=True) + jnp.exp(sink - m))
            o = jnp.dot(p.astype(BF16), vb, preferred_element_type=F32)
            outs += [o[g * blk:(g + 1) * blk] for g in range(qpk)]
        o_ref[0, jb * blk:(jb + 1) * blk, :] = jnp.concatenate(outs, axis=1).astype(o_ref.dtype)


def _window_attention(q, k, v, sinks):
    B, S, HD = q.shape
    KD = k.shape[-1]
    tq = min(ATTN_TILE, S)
    bpt = tq // WINDOW
    cur = lambda w: pl.BlockSpec((1, tq, w), lambda b, i: (b, i, 0))
    prev = lambda w: pl.BlockSpec((1, WINDOW, w), lambda b, i: (b, jnp.maximum(i * bpt - 1, 0), 0))
    return pl.pallas_call(
        functools.partial(_attn_kernel, n_heads=HD // HEAD_DIM),
        grid=(B, S // tq),
        in_specs=[pl.BlockSpec(memory_space=pltpu.SMEM),
                  cur(HD), cur(KD), prev(KD), cur(KD), prev(KD)],
        out_specs=cur(HD),
        out_shape=jax.ShapeDtypeStruct((B, S, HD), BF16),
        compiler_params=_cparams(("parallel", "parallel")),
        name="window_attn",
    )(sinks.astype(F32), q, k, k, v, v)


def _attn_tail_kernel(a_ref, x_ref, cond_ref, wo_ref, g_ref, xo_ref, h_ref, *, gate_row, shift_row, scale_row):
    o = jnp.dot(a_ref[0], wo_ref[...], preferred_element_type=F32)
    xn = x_ref[0] + cond_ref[0, gate_row:gate_row + 1, :] * o
    xo_ref[0] = xn
    h_ref[0] = _norm_mod(xn, g_ref[...], cond_ref[0, shift_row:shift_row + 1, :],
                         cond_ref[0, scale_row:scale_row + 1, :]).astype(h_ref.dtype)


def _attn_tail(a, x, cond, w_o, g_ffn, gate_row, shift_row, scale_row):
    B, S, D = x.shape
    HD = a.shape[-1]
    tm = min(TOKEN_TILE, S)
    tok = lambda w: pl.BlockSpec((1, tm, w), lambda b, i: (b, i, 0))
    return pl.pallas_call(
        functools.partial(_attn_tail_kernel, gate_row=gate_row, shift_row=shift_row, scale_row=scale_row),
        grid=(B, S // tm),
        in_specs=[tok(HD), tok(D),
                  pl.BlockSpec((1, cond.shape[1], D), lambda b, i: (b, 0, 0)),
                  pl.BlockSpec((HD, D), lambda b, i: (0, 0)),
                  pl.BlockSpec((1, D), lambda b, i: (0, 0))],
        out_specs=[tok(D), tok(D)],
        out_shape=[jax.ShapeDtypeStruct((B, S, D), F32), jax.ShapeDtypeStruct((B, S, D), BF16)],
        compiler_params=_cparams(("parallel", "parallel")),
        name="attn_tail",
    )(a, x, cond, w_o.astype(BF16), g_ffn.reshape(1, D))


def _final_kernel(x_ref, cond_ref, g_ref, o_ref, *, shift_row, scale_row):
    o_ref[0] = _norm_mod(x_ref[0], g_ref[...], cond_ref[0, shift_row:shift_row + 1, :],
                         cond_ref[0, scale_row:scale_row + 1, :])


def _final_norm(x, cond, g, shift_row, scale_row):
    B, S, D = x.shape
    tm = min(TOKEN_TILE, S)
    tok = pl.BlockSpec((1, tm, D), lambda b, i: (b, i, 0))
    return pl.pallas_call(
        functools.partial(_final_kernel, shift_row=shift_row, scale_row=scale_row),
        grid=(B, S // tm),
        in_specs=[tok, pl.BlockSpec((1, cond.shape[1], D), lambda b, i: (b, 0, 0)),
                  pl.BlockSpec((1, D), lambda b, i: (0, 0))],
        out_specs=tok,
        out_shape=jax.ShapeDtypeStruct((B, S, D), F32),
        compiler_params=_cparams(("parallel", "parallel")),
        name="final_norm",
    )(x, cond, g.reshape(1, D))


def _rope_tables(positions):
    inv_freq = ROPE_THETA ** (-jnp.arange(0, HEAD_DIM, 2, dtype=F32) / HEAD_DIM)
    ang = positions.astype(F32)[..., None] * inv_freq
    reps = LANES // (HEAD_DIM // 2)
    return jnp.tile(jnp.cos(ang), (1, 1, reps)), jnp.tile(jnp.sin(ang), (1, 1, reps))


def _rotated_half_columns(w):
    d, n = w.shape
    w4 = w.reshape(d, n // HEAD_DIM, 2, HEAD_DIM // 2)
    return jnp.stack([-w4[:, :, 1], w4[:, :, 0]], axis=2).reshape(d, n)


def kernel(x, c, positions, w_cond, b_cond, norm_mix, norm_ffn, s5_w_in, s5_a_re, s5_a_im, s5_log_dt, s5_b_re, s5_b_im, s5_c_re, s5_c_im, s5_d, s5_w_glu, s5_b_glu, s5_w_out, kv_norm, w_kv, attn_w_q, attn_sinks, attn_w_o, peer_w_q, peer_sub_keys, peer_u, peer_v, final_norm):
    B, S, D = x.shape
    depth = norm_mix.shape[0]
    n_a = s5_w_in.shape[0]
    x = x.astype(F32)
    n_cond = w_cond.shape[1] // D
    cond = _cond_proj(c, w_cond, b_cond).reshape(B, n_cond, D)
    cos, sin = _rope_tables(positions)
    kv_rows = N_COND_PER_LAYER * depth
    k_sh = v_sh = None
    for i in range(depth):
        r0 = N_COND_PER_LAYER * i
        if i < n_a:
            a = i
            (u,) = _norm_proj(x, cond, norm_mix[i], s5_w_in[a].astype(BF16), r0, r0 + 1,
                              [s5_w_in.shape[2]], [F32], name="s5_in_proj")
            tables = _s5_tables(s5_a_re[a], s5_a_im[a], s5_log_dt[a], s5_b_re[a], s5_b_im[a],
                                s5_c_re[a], s5_c_im[a], S // S5_CHUNK)
            y = _s5_core(u, tables)
            x, h = _s5_tail(y, u, x, cond, s5_d[a].reshape(-1), s5_w_glu[a], s5_b_glu[a], s5_w_out[a],
                            norm_ffn[i], r0 + 2, r0 + 3, r0 + 4)
        else:
            b = i - n_a
            wq = attn_w_q[b]
            wq2 = jnp.concatenate([wq, _rotated_half_columns(wq)], axis=1).astype(BF16)
            (q,) = _norm_proj(x, cond, norm_mix[i], wq2, r0, r0 + 1, [wq.shape[1]], [BF16],
                              rope=(cos, sin), name="q_proj")
            att = _window_attention(q, k_sh, v_sh, attn_sinks[b])
            x, h = _attn_tail(att, x, cond, attn_w_o[b], norm_ffn[i], r0 + 2, r0 + 3, r0 + 4)
        x = _peer_layer(h, x, cond, peer_w_q[i], peer_sub_keys[i], peer_u[i], peer_v[i], r0 + 5)
        if i == n_a - 1:
            kd = w_kv.shape[1] // 2
            wk = w_kv[:, :kd]
            wkv = jnp.concatenate([wk, _rotated_half_columns(wk), w_kv[:, kd:]], axis=1).astype(BF16)
            k_sh, v_sh = _norm_proj(x, cond, kv_norm, wkv, kv_rows, kv_rows + 1, [kd, kd], [BF16, BF16],
                                    rope=(cos, sin), name="kv_proj")
    return _final_norm(x, cond, final_norm, kv_rows + 2, kv_rows + 3)
```
